```python
import math
import jax, jax.numpy as jnp
from jax import lax
import numpy as np

D_MODEL = 1024
BATCH = 8
SEQ = 2048
DEPTH = 4
DEC_BATCH = 128
DEC_SEQ = 1
PAST_LEN = 16384
PAGE_SIZE = 128

N_AB = (DEPTH + 1) // 2
N_C = DEPTH // 2
D_S5 = D_MODEL // 2
S5_GROUP = 16
S5_GROUPS = D_S5 // S5_GROUP
S5_STATE = 64
D_HG = D_MODEL // 2
HG_DK = 128
HG_HEADS = D_HG // HG_DK
HG_DV = D_HG // HG_HEADS
HG_CHUNK = 16
D_AB_IN = D_S5 + 4 * D_HG
D_CONF = D_MODEL
CONF_W = 31
N_MEM = 256
X_HEADS = 4
X_HEAD_DIM = D_MODEL // X_HEADS
D_FF = ((8 * D_MODEL // 3 + 127) // 128) * 128
FFN_W = 3
EPS = 1e-6
N_NORMS = 7
NG_MIX_PRE, NG_MIX_POST, NG_X_PRE, NG_X_POST, NG_FFN_PRE, NG_FFN_POST, NG_MEM = range(N_NORMS)

kernel_name = 'hybrid_s5_hgrn2_conformer_decoder_step'


def _rms_norm(x, g):
    xf = x.astype(jnp.float32)
    y = xf * lax.rsqrt(jnp.mean(xf * xf, axis=-1, keepdims=True) + EPS)
    return (y * g.astype(jnp.float32)).astype(x.dtype)


def _layer_norm(x, g, b):
    xf = x.astype(jnp.float32)
    mu = jnp.mean(xf, axis=-1, keepdims=True)
    xc = xf - mu
    y = xc * lax.rsqrt(jnp.mean(xc * xc, axis=-1, keepdims=True) + EPS)
    return (y * g.astype(jnp.float32) + b.astype(jnp.float32)).astype(x.dtype)


def _causal_dwconv(u, buf, w, b):
    n_tap = w.shape[0]
    ext = jnp.concatenate([buf.astype(u.dtype), u], axis=1)
    out = lax.conv_general_dilated(ext, w.astype(u.dtype)[:, None, :], (1,), 'VALID',
                                   dimension_numbers=('NWC', 'WIO', 'NWC'),
                                   feature_group_count=u.shape[-1])
    return out + b.astype(u.dtype), ext[:, ext.shape[1] - (n_tap - 1):]


def _s5(u, h_re, h_im, lam_re, lam_im, b_re, b_im, c_re, c_im, d, log_dt, w_glu, b_glu):
    f32 = jnp.float32
    n, l, _ = u.shape
    uf = u.astype(f32).reshape(n, l, S5_GROUPS, S5_GROUP)
    dt = jnp.exp(log_dt.astype(f32))[:, None]
    lr, li = lam_re.astype(f32), lam_im.astype(f32)
    mag = jnp.exp(lr * dt)
    ar, ai = mag * jnp.cos(li * dt), mag * jnp.sin(li * dt)
    den = lr * lr + li * li
    er = ((ar - 1.0) * lr + ai * li) / den
    ei = (ai * lr - (ar - 1.0) * li) / den
    br, bi = b_re.astype(f32), b_im.astype(f32)
    bbr = er[..., None] * br - ei[..., None] * bi
    bbi = er[..., None] * bi + ei[..., None] * br
    bu_r = jnp.einsum('gpc,nlgc->nlgp', bbr, uf)
    bu_i = jnp.einsum('gpc,nlgc->nlgp', bbi, uf)
    hr0, hi0 = h_re.astype(f32), h_im.astype(f32)
    bu_r = bu_r.at[:, 0].add(ar * hr0 - ai * hi0)
    bu_i = bu_i.at[:, 0].add(ar * hi0 + ai * hr0)
    a_r = jnp.broadcast_to(ar, bu_r.shape)
    a_i = jnp.broadcast_to(ai, bu_i.shape)

    def combine(e1, e2):
        a1r, a1i, b1r, b1i = e1
        a2r, a2i, b2r, b2i = e2
        return (a1r * a2r - a1i * a2i, a1r * a2i + a1i * a2r,
                a2r * b1r - a2i * b1i + b2r, a2r * b1i + a2i * b1r + b2i)

    _, _, xr, xi = lax.associative_scan(combine, (a_r, a_i, bu_r, bu_i), axis=1)
    y = (jnp.einsum('gcp,nlgp->nlgc', c_re.astype(f32), xr)
         - jnp.einsum('gcp,nlgp->nlgc', c_im.astype(f32), xi)).reshape(n, l, D_S5)
    y = y + d.astype(f32) * u.astype(f32)
    z = jax.nn.gelu(y)
    out = z * jax.nn.sigmoid(z @ w_glu.astype(f32) + b_glu.astype(f32))
    return out, xr[:, -1], xi[:, -1]


def _hgrn2(q, f_logit, i_in, g_out, s0, lb, gnorm):
    f32 = jnp.float32
    n, l, _ = q.shape
    f = lb + (1.0 - lb) * jax.nn.sigmoid(f_logit.astype(f32))
    log_f = jnp.log(f).reshape(n, l, HG_HEADS, HG_DK)
    k = (1.0 - f).reshape(n, l, HG_HEADS, HG_DK)
    qf = jax.nn.silu(q.astype(f32)).reshape(n, l, HG_HEADS, HG_DK)
    v = i_in.astype(f32).reshape(n, l, HG_HEADS, HG_DV)
    n_chunk = -(-l // HG_CHUNK)
    pad = n_chunk * HG_CHUNK - l

    def to_chunks(t):
        t = jnp.pad(t, ((0, 0), (0, pad), (0, 0), (0, 0)))
        return t.reshape(n, n_chunk, HG_CHUNK, HG_HEADS, -1).transpose(1, 0, 3, 2, 4)

    causal = jnp.tril(jnp.ones((HG_CHUNK, HG_CHUNK), bool))[:, :, None]

    def step(s, inp):
        qc, lc, kc, vc = inp
        b = jnp.cumsum(lc, axis=2)
        diff = b[:, :, :, None, :] - b[:, :, None, :, :]
        decay = jnp.exp(jnp.where(causal, diff, -jnp.inf))
        att = jnp.einsum('nhtd,nhsd,nhtsd->nhts', qc, kc, decay)
        o = (jnp.einsum('nhts,nhsv->nhtv', att, vc)
             + jnp.einsum('nhtd,nhdv->nhtv', qc * jnp.exp(b), s))
        b_end = b[:, :, -1:, :]
        s_new = (jnp.exp(b_end[:, :, 0, :])[..., None] * s
                 + jnp.einsum('nhsd,nhsv->nhdv', kc * jnp.exp(b_end - b), vc))
        return s_new, o

    s_fin, o = lax.scan(step, s0.astype(f32),
                        (to_chunks(qf), to_chunks(log_f), to_chunks(k), to_chunks(v)))
    o = o.transpose(1, 0, 3, 2, 4).reshape(n, n_chunk * HG_CHUNK, HG_HEADS, HG_DV)[:, :l]
    o = o * lax.rsqrt(jnp.mean(o * o, axis=-1, keepdims=True) + EPS) * gnorm.astype(f32).reshape(HG_HEADS, HG_DV)
    o = o * jax.nn.sigmoid(g_out.astype(f32)).reshape(n, l, HG_HEADS, HG_DV)
    return o.reshape(n, l, D_HG), s_fin


def _conformer_conv(z, buf, w_dw, b_dw, ln_g, ln_b):
    a, gate = jnp.split(z, 2, axis=-1)
    u = a * jax.nn.sigmoid(gate)
    c, new_buf = _causal_dwconv(u, buf, w_dw, b_dw)
    return jax.nn.silu(_layer_norm(c, ln_g, ln_b)), new_buf


def _cross_attn(h, k, v, wq, wo):
    n, l, _ = h.shape
    q = (h @ wq).reshape(n, l, X_HEADS, X_HEAD_DIM)
    s = jnp.einsum('nlhd,nmhd->nhlm', q, k.astype(q.dtype)).astype(jnp.float32) * (X_HEAD_DIM ** -0.5)
    p = jax.nn.softmax(s, axis=-1).astype(v.dtype)
    o = jnp.einsum('nhlm,nmhd->nlhd', p, v).reshape(n, l, D_MODEL)
    return o @ wo


def _conv_ffn(h, buf, w_in, w_dw, b_dw, w_out):
    a, b = jnp.split(h @ w_in, 2, axis=-1)
    a, new_buf = _causal_dwconv(a, buf, w_dw, b_dw)
    return (jax.nn.gelu(a) * b) @ w_out, new_buf


def _trunk(x, mem_k, mem_v, s5_re, s5_im, hg_s, conf_buf, ffn_buf, w):
    new_re, new_im, new_hg, new_conf, new_ffn = [], [], [], [], []
    for li in range(DEPTH):
        g = w['norm_gains'][li]
        j = li // 2
        h = _rms_norm(x, g[NG_MIX_PRE])
        if li % 2 == 0:
            z = h @ w['w_ab_in'][j]
            u, q, fl, iv, og = jnp.split(z, [D_S5, D_S5 + D_HG, D_S5 + 2 * D_HG, D_S5 + 3 * D_HG], axis=-1)
            ya, hr, hi = _s5(u, s5_re[j], s5_im[j], w['s5_lambda_re'][j], w['s5_lambda_im'][j],
                             w['s5_b_re'][j], w['s5_b_im'][j], w['s5_c_re'][j], w['s5_c_im'][j],
                             w['s5_d'][j], w['s5_log_dt'][j], w['s5_w_glu'][j], w['s5_b_glu'][j])
            yb, hs = _hgrn2(q, fl, iv, og, hg_s[j], w['hg_lb'][j], w['hg_gnorm'][j])
            y = jnp.concatenate([ya.astype(h.dtype), yb.astype(h.dtype)], axis=-1) @ w['w_ab_out'][j]
            new_re.append(hr)
            new_im.append(hi)
            new_hg.append(hs)
        else:
            z = h @ w['w_conf_in'][j]
            c, cb = _conformer_conv(z, conf_buf[j], w['conf_dw'][j], w['conf_dw_b'][j],
                                    w['conf_ln_g'][j], w['conf_ln_b'][j])
            y = c @ w['w_conf_out'][j]
            new_conf.append(cb)
        x = x + _rms_norm(y, g[NG_MIX_POST]).astype(x.dtype)
        a = _cross_attn(_rms_norm(x, g[NG_X_PRE]), mem_k[li], mem_v[li], w['w_xq'][li], w['w_xo'][li])
        x = x + _rms_norm(a, g[NG_X_POST]).astype(x.dtype)
        f, fb = _conv_ffn(_rms_norm(x, g[NG_FFN_PRE]), ffn_buf[li], w['w_ffn_in'][li],
                          w['ffn_dw'][li], w['ffn_dw_b'][li], w['w_ffn_out'][li])
        x = x + _rms_norm(f, g[NG_FFN_POST]).astype(x.dtype)
        new_ffn.append(fb)
    return (x, jnp.stack(new_re), jnp.stack(new_im), jnp.stack(new_hg),
            jnp.stack(new_conf), jnp.stack(new_ffn))


def setup_inputs(seed: int = 0) -> dict:
    key = jax.random.key(seed)
    ks = iter(jax.random.split(key, 64))
    f32 = jnp.float32

    def nrm(shape, scale):
        return scale * jax.random.normal(next(ks), shape, f32)

    n_idx = jnp.arange(S5_STATE, dtype=f32)
    return {
        'x_prompt': nrm((BATCH, SEQ, D_MODEL), 1.0),
        'x_sample': nrm((DEC_BATCH, DEC_SEQ, D_MODEL), 1.0),
        'cache_mem_k': nrm((DEPTH, DEC_BATCH, N_MEM, X_HEADS, X_HEAD_DIM), 1.0),
        'cache_mem_v': nrm((DEPTH, DEC_BATCH, N_MEM, X_HEADS, X_HEAD_DIM), 1.0),
        'state_s5_re': nrm((N_AB, DEC_BATCH, S5_GROUPS, S5_STATE), 0.1),
        'state_s5_im': nrm((N_AB, DEC_BATCH, S5_GROUPS, S5_STATE), 0.1),
        'state_hgrn': nrm((N_AB, DEC_BATCH, HG_HEADS, HG_DK, HG_DV), 0.5),
        'state_conf': nrm((N_C, DEC_BATCH, CONF_W - 1, D_CONF), 0.5),
        'state_ffn': nrm((DEPTH, DEC_BATCH, FFN_W - 1, D_FF), 1.0),
        'mem_prompt': nrm((BATCH, N_MEM, D_MODEL), 1.0),
        'norm_gains': 1.0 + nrm((DEPTH, N_NORMS, D_MODEL), 0.05),
        'w_ab_in': nrm((N_AB, D_MODEL, D_AB_IN), D_MODEL ** -0.5),
        'w_ab_out': nrm((N_AB, D_S5 + D_HG, D_MODEL), (D_S5 + D_HG) ** -0.5),
        's5_lambda_re': -0.5 + nrm((N_AB, S5_GROUPS, S5_STATE), 0.01),
        's5_lambda_im': math.pi * n_idx + nrm((N_AB, S5_GROUPS, S5_STATE), 0.01),
        's5_b_re': nrm((N_AB, S5_GROUPS, S5_STATE, S5_GROUP), (2 * S5_GROUP) ** -0.5),
        's5_b_im': nrm((N_AB, S5_GROUPS, S5_STATE, S5_GROUP), (2 * S5_GROUP) ** -0.5),
        's5_c_re': nrm((N_AB, S5_GROUPS, S5_GROUP, S5_STATE), S5_STATE ** -0.5),
        's5_c_im': nrm((N_AB, S5_GROUPS, S5_GROUP, S5_STATE), S5_STATE ** -0.5),
        's5_d': nrm((N_AB, D_S5), 1.0),
        's5_log_dt': jax.random.uniform(next(ks), (N_AB, S5_GROUPS), f32, math.log(1e-3), math.log(1e-1)),
        's5_w_glu': nrm((N_AB, D_S5, D_S5), D_S5 ** -0.5),
        's5_b_glu': nrm((N_AB, D_S5), 0.01),
        'hg_lb_logits': nrm((N_AB, D_HG), 0.1),
        'hg_gnorm': 1.0 + nrm((N_AB, D_HG), 0.05),
        'w_conf_in': nrm((N_C, D_MODEL, 2 * D_CONF), D_MODEL ** -0.5),
        'conf_dw': nrm((N_C, CONF_W, D_CONF), CONF_W ** -0.5),
        'conf_dw_b': nrm((N_C, D_CONF), 0.01),
        'conf_ln_g': 1.0 + nrm((N_C, D_CONF), 0.05),
        'conf_ln_b': nrm((N_C, D_CONF), 0.01),
        'w_conf_out': nrm((N_C, D_CONF, D_MODEL), D_CONF ** -0.5),
        'w_xq': nrm((DEPTH, D_MODEL, D_MODEL), D_MODEL ** -0.5),
        'w_xk': nrm((DEPTH, D_MODEL, D_MODEL), D_MODEL ** -0.5),
        'w_xv': nrm((DEPTH, D_MODEL, D_MODEL), D_MODEL ** -0.5),
        'w_xo': nrm((DEPTH, D_MODEL, D_MODEL), D_MODEL ** -0.5),
        'w_ffn_in': nrm((DEPTH, D_MODEL, 2 * D_FF), D_MODEL ** -0.5),
        'ffn_dw': nrm((DEPTH, FFN_W, D_FF), FFN_W ** -0.5),
        'ffn_dw_b': nrm((DEPTH, D_FF), 0.01),
        'w_ffn_out': nrm((DEPTH, D_FF, D_MODEL), D_FF ** -0.5),
    }


def reference(x_prompt, x_sample, cache_mem_k, cache_mem_v, state_s5_re, state_s5_im, state_hgrn,
              state_conf, state_ffn, mem_prompt, norm_gains, w_ab_in, w_ab_out, s5_lambda_re,
              s5_lambda_im, s5_b_re, s5_b_im, s5_c_re, s5_c_im, s5_d, s5_log_dt, s5_w_glu, s5_b_glu,
              hg_lb_logits, hg_gnorm, w_conf_in, conf_dw, conf_dw_b, conf_ln_g, conf_ln_b, w_conf_out,
              w_xq, w_xk, w_xv, w_xo, w_ffn_in, ffn_dw, ffn_dw_b, w_ffn_out):
    f32 = jnp.float32
    sm = jax.nn.softmax(hg_lb_logits.astype(f32), axis=0)
    hg_lb = jnp.cumsum(sm, axis=0) - sm[0]
    w = dict(norm_gains=norm_gains, w_ab_in=w_ab_in, w_ab_out=w_ab_out,
             s5_lambda_re=s5_lambda_re, s5_lambda_im=s5_lambda_im, s5_b_re=s5_b_re, s5_b_im=s5_b_im,
             s5_c_re=s5_c_re, s5_c_im=s5_c_im, s5_d=s5_d, s5_log_dt=s5_log_dt,
             s5_w_glu=s5_w_glu, s5_b_glu=s5_b_glu, hg_lb=hg_lb, hg_gnorm=hg_gnorm,
             w_conf_in=w_conf_in, conf_dw=conf_dw, conf_dw_b=conf_dw_b, conf_ln_g=conf_ln_g,
             conf_ln_b=conf_ln_b, w_conf_out=w_conf_out, w_xq=w_xq, w_xo=w_xo,
             w_ffn_in=w_ffn_in, ffn_dw=ffn_dw, ffn_dw_b=ffn_dw_b, w_ffn_out=w_ffn_out)

    nb, n_mem, _ = mem_prompt.shape
    mk, mv = [], []
    for li in range(DEPTH):
        m = _rms_norm(mem_prompt, norm_gains[li, NG_MEM])
        mk.append((m @ w_xk[li]).reshape(nb, n_mem, X_HEADS, X_HEAD_DIM))
        mv.append((m @ w_xv[li]).reshape(nb, n_mem, X_HEADS, X_HEAD_DIM))
    p_mem_k = jnp.stack(mk)
    p_mem_v = jnp.stack(mv)

    nbp = x_prompt.shape[0]
    z_re = jnp.zeros((N_AB, nbp, S5_GROUPS, S5_STATE), f32)
    z_hg = jnp.zeros((N_AB, nbp, HG_HEADS, HG_DK, HG_DV), f32)
    z_conf = jnp.zeros((N_C, nbp, CONF_W - 1, D_CONF), x_prompt.dtype)
    z_ffn = jnp.zeros((DEPTH, nbp, FFN_W - 1, D_FF), x_prompt.dtype)
    y_prompt, p_re, p_im, p_hg, p_conf, p_ffn = _trunk(
        x_prompt, p_mem_k, p_mem_v, z_re, z_re, z_hg, z_conf, z_ffn, w)

    y_sample, s_re, s_im, s_hg, s_conf, s_ffn = _trunk(
        x_sample, cache_mem_k, cache_mem_v, state_s5_re, state_s5_im, state_hgrn,
        state_conf, state_ffn, w)

    return (y_prompt, y_sample, p_re, p_im, p_hg, p_conf, p_ffn, p_mem_k, p_mem_v,
            s_re, s_im, s_hg, s_conf, s_ffn)
```

```python
import functools
import math

import jax
import jax.numpy as jnp
from jax import lax
from jax.experimental import pallas as pl
from jax.experimental.pallas import tpu as pltpu

F32 = jnp.float32
BF16 = jnp.bfloat16
EPS = 1e-6

N_NORMS = 7
NG_MIX_PRE, NG_MIX_POST, NG_X_PRE, NG_X_POST, NG_FFN_PRE, NG_FFN_POST, NG_MEM = range(N_NORMS)

S5_GROUP = 16
S5_STATE = 64
HG_DK = 128
CONF_W = 31
FFN_W = 3
X_HEADS = 4

LANES = 128
SUBLANES = 8
VMEM_LIMIT = 56 * 1024 * 1024

ROW_TILE = 512
HG_CHUNK = 64
HG_DIAG = 16
S5_CHUNK = 64
S5_LANES = 512
FFN_COLS = 256
CONV_ROWS = 16


def _cparams(n_axes=1):
    return pltpu.CompilerParams(dimension_semantics=("arbitrary",) * n_axes,
                                vmem_limit_bytes=VMEM_LIMIT)


def _rms(x, g):
    return x * lax.rsqrt(jnp.mean(x * x, axis=-1, keepdims=True) + EPS) * g


def _gelu(x):
    return 0.5 * x * (1.0 + jnp.tanh(0.7978845608028654 * (x + 0.044715 * (x * x * x))))


def _sigmoid(x):
    return 1.0 / (1.0 + jnp.exp(-x))


def _dot(a, b):
    return jnp.dot(a, b, preferred_element_type=F32)


def _full_spec(arr):
    nd = arr.ndim
    return pl.BlockSpec(arr.shape, lambda *_: (0,) * nd)


def _resident_spec(arr):
    nd = arr.ndim
    return pl.BlockSpec(arr.shape, lambda *_: (0,) * nd, pipeline_mode=pl.Buffered(1))


def _norm_matmul_kernel(x_ref, g_ref, w_ref, *o_refs, col_slices, glu):
    h = _rms(x_ref[...], g_ref[...]).astype(BF16)
    n_half = w_ref.shape[1] // 2
    for o_ref, (c0, cw) in zip(o_refs, col_slices):
        z = _dot(h, w_ref[:, c0:c0 + cw])
        if glu:
            z = z * _sigmoid(_dot(h, w_ref[:, n_half + c0:n_half + c0 + cw]))
        o_ref[...] = z.astype(o_ref.dtype)


def _norm_matmul(x2, g, w, outs, *, tm, glu=False):
    rows, d = x2.shape
    col_slices = tuple((o[0], o[1]) for o in outs)
    return pl.pallas_call(
        functools.partial(_norm_matmul_kernel, col_slices=col_slices, glu=glu),
        grid=(rows // tm,),
        in_specs=[pl.BlockSpec((tm, d), lambda i: (i, 0)),
                  _full_spec(g), _full_spec(w)],
        out_specs=[pl.BlockSpec(o[3], o[4]) for o in outs],
        out_shape=[jax.ShapeDtypeStruct(o[2], o[5]) for o in outs],
        compiler_params=_cparams(),
        name="norm_matmul",
    )(x2, g, w)


def _proj_res_kernel(*refs, n_in):
    a_refs, w_refs = refs[:n_in], refs[n_in:2 * n_in]
    x_ref, g_ref, o_ref = refs[2 * n_in:]
    y = _dot(a_refs[0][...].astype(BF16), w_refs[0][...])
    for a_ref, w_ref in zip(a_refs[1:], w_refs[1:]):
        y = y + _dot(a_ref[...].astype(BF16), w_ref[...])
    o_ref[...] = x_ref[...] + _rms(y, g_ref[...])


def _proj_res(a_list, w_list, x2, g, *, tm):
    rows, d = x2.shape
    n_in = len(a_list)
    return pl.pallas_call(
        functools.partial(_proj_res_kernel, n_in=n_in),
        grid=(rows // tm,),
        in_specs=([pl.BlockSpec(a[1], a[2]) for a in a_list]
                  + [_full_spec(w) for w in w_list]
                  + [pl.BlockSpec((tm, d), lambda i: (i, 0)), _full_spec(g)]),
        out_specs=pl.BlockSpec((tm, d), lambda i: (i, 0)),
        out_shape=jax.ShapeDtypeStruct((rows, d), F32),
        compiler_params=_cparams(),
        name="proj_res",
    )(*[a[0] for a in a_list], *w_list, x2, g)


def _s5_disc_kernel(lr_ref, li_ref, ldt_ref, br_ref, bi_ref, ar_ref, ai_ref, bbr_ref, bbi_ref):
    lr, li = lr_ref[...], li_ref[...]
    dt = jnp.exp(ldt_ref[...])
    mag = jnp.exp(lr * dt)
    ar, ai = mag * jnp.cos(li * dt), mag * jnp.sin(li * dt)
    den = lr * lr + li * li
    er = ((ar - 1.0) * lr + ai * li) / den
    ei = (ai * lr - (ar - 1.0) * li) / den
    br, bi = br_ref[...], bi_ref[...]
    ar_ref[...] = ar
    ai_ref[...] = ai
    bbr_ref[...] = er * br - ei * bi
    bbi_ref[...] = er * bi + ei * br


def _s5_discretise(lam_re, lam_im, log_dt, b_re, b_im):
    g, p = lam_re.shape
    c = b_re.shape[-1]
    gp = g * p
    args = (lam_re.reshape(1, gp), lam_im.reshape(1, gp),
            jnp.repeat(log_dt, p).reshape(1, gp),
            b_re.transpose(2, 0, 1).reshape(c, gp), b_im.transpose(2, 0, 1).reshape(c, gp))
    row = jax.ShapeDtypeStruct((1, gp), F32)
    mat = jax.ShapeDtypeStruct((c, gp), F32)
    return pl.pallas_call(
        _s5_disc_kernel,
        in_specs=[_full_spec(a) for a in args],
        out_specs=[pl.BlockSpec((1, gp), lambda: (0, 0))] * 2 + [pl.BlockSpec((c, gp), lambda: (0, 0))] * 2,
        out_shape=[row, row, mat, mat],
        name="s5_discretise",
    )(*args)


def _s5_block_weights(bbr_t, bbi_t, c_re, c_im):
    g, c, p = c_re.shape
    gps = LANES // c
    n_slab = g // gps
    eye = jnp.eye(gps, dtype=F32)

    def in_w(bt):
        b4 = bt.reshape(c, n_slab, gps, p).transpose(1, 2, 0, 3)
        return jnp.einsum('jgcp,gh->jgchp', b4, eye).reshape(n_slab, gps * c, gps * p)

    def out_w(cm):
        c4 = cm.reshape(n_slab, gps, c, p)
        return jnp.einsum('jgcp,gh->jgphc', c4, eye).reshape(n_slab, gps * p, gps * c)

    wb = jnp.concatenate([in_w(bbr_t), in_w(bbi_t)], axis=-1).astype(BF16)
    return wb, out_w(c_re).astype(BF16), out_w(c_im).astype(BF16)


def _s5_kernel(u_ref, h0r_ref, h0i_ref, ar_ref, ai_ref, wb_ref, wcr_ref, wci_ref, d_ref,
               wglu_ref, bglu_ref, y_ref, hr_ref, hi_ref, xr_s, xi_s, cr_s, ci_s, y_s, *, nb, tc):
    n_slab, _, two_sc = wb_ref.shape
    sc = two_sc // 2
    n_state = n_slab * sc

    @pl.when(pl.program_id(0) == 0)
    def _():
        cr_s[...] = h0r_ref[...]
        ci_s[...] = h0i_ref[...]

    u = u_ref[...]
    ub = u.astype(BF16)
    for j in range(n_slab):
        bu = _dot(ub[:, j * LANES:(j + 1) * LANES], wb_ref[j])
        xr_s[:, j * sc:(j + 1) * sc] = bu[:, :sc]
        xi_s[:, j * sc:(j + 1) * sc] = bu[:, sc:]

    for rt in range(nb // SUBLANES):
        for cb in range(n_state // S5_LANES):
            rows = slice(rt * SUBLANES, (rt + 1) * SUBLANES)
            cols = slice(cb * S5_LANES, (cb + 1) * S5_LANES)
            ar = jnp.broadcast_to(ar_ref[:, cols], (SUBLANES, S5_LANES))
            ai = jnp.broadcast_to(ai_ref[:, cols], (SUBLANES, S5_LANES))

            def step(t, carry, cols=cols, rt=rt, ar=ar, ai=ai):
                xr, xi = carry
                r0 = pl.multiple_of(t * nb + rt * SUBLANES, SUBLANES)
                nr = (ar * xr - ai * xi) + xr_s[pl.ds(r0, SUBLANES), cols]
                ni = (ar * xi + ai * xr) + xi_s[pl.ds(r0, SUBLANES), cols]
                xr_s[pl.ds(r0, SUBLANES), cols] = nr
                xi_s[pl.ds(r0, SUBLANES), cols] = ni
                return nr, ni

            carry = (cr_s[rows, cols], ci_s[rows, cols])
            if tc == 1:
                carry = step(0, carry)
            else:
                carry = lax.fori_loop(0, tc, step, carry, unroll=4)
            cr_s[rows, cols] = carry[0]
            ci_s[rows, cols] = carry[1]

    hr_ref[...] = cr_s[...]
    hi_ref[...] = ci_s[...]

    for j in range(n_slab):
        xr = xr_s[:, j * sc:(j + 1) * sc].astype(BF16)
        xi = xi_s[:, j * sc:(j + 1) * sc].astype(BF16)
        y_s[:, j * LANES:(j + 1) * LANES] = _dot(xr, wcr_ref[j]) - _dot(xi, wci_ref[j])
    y = y_s[...] + d_ref[...] * u
    z = _gelu(y)
    gate = _dot(z.astype(BF16), wglu_ref[...]) + bglu_ref[...]
    y_ref[...] = z * _sigmoid(gate)


def _s5(u_tm, h0r, h0i, ar, ai, wb, wcr, wci, d, wglu, bglu, *, nb, tc):
    rows, ch = u_tm.shape
    n_state = h0r.shape[1]
    blk = tc * nb
    small = [h0r, h0i, ar, ai, wb, wcr, wci, d, wglu, bglu]
    state = jax.ShapeDtypeStruct((nb, n_state), F32)
    return pl.pallas_call(
        functools.partial(_s5_kernel, nb=nb, tc=tc),
        grid=(rows // blk,),
        in_specs=[pl.BlockSpec((blk, ch), lambda i: (i, 0))] + [_full_spec(a) for a in small],
        out_specs=[pl.BlockSpec((blk, ch), lambda i: (i, 0)),
                   pl.BlockSpec((nb, n_state), lambda i: (0, 0)),
                   pl.BlockSpec((nb, n_state), lambda i: (0, 0))],
        out_shape=[jax.ShapeDtypeStruct((rows, ch), F32), state, state],
        scratch_shapes=[pltpu.VMEM((blk, n_state), F32), pltpu.VMEM((blk, n_state), F32),
                        pltpu.VMEM((nb, n_state), F32), pltpu.VMEM((nb, n_state), F32),
                        pltpu.VMEM((blk, ch), F32)],
        compiler_params=_cparams(),
        name="s5",
    )(u_tm, *small)


def _hgrn_kernel(z_ref, s0_ref, lbl_ref, gn_ref, o_ref, sout_ref, st_s, o_s, *,
                 layer, t_chunk, n_chunk, valid_len):
    c = pl.program_id(0) % n_chunk
    n_head = s0_ref.shape[1]
    dk = s0_ref.shape[2]
    d_hg = n_head * dk
    t = t_chunk

    @pl.when(c == 0)
    def _():
        for h in range(n_head):
            st_s[h] = s0_ref[0, h].T

    lg = lbl_ref[...]
    e = jnp.exp(lg - jnp.max(lg, axis=0, keepdims=True))
    sm = e / jnp.sum(e, axis=0, keepdims=True)
    lb = jnp.sum(sm[:layer + 1], axis=0, keepdims=True) - sm[0:1]

    q_in = z_ref[:, 0:d_hg]
    f_in = z_ref[:, d_hg:2 * d_hg]
    v_all = z_ref[:, 2 * d_hg:3 * d_hg]
    g_in = z_ref[:, 3 * d_hg:4 * d_hg]
    f = lb + (1.0 - lb) * _sigmoid(f_in)
    logf = jnp.log(f)
    k_all = 1.0 - f
    if valid_len < t:
        live = lax.broadcasted_iota(jnp.int32, (t, 1), 0) < valid_len
        logf = jnp.where(live, logf, 0.0)
        k_all = jnp.where(live, k_all, 0.0)
    q_all = q_in * _sigmoid(q_in)
    tri = (lax.broadcasted_iota(jnp.int32, (t, t), 0)
           >= lax.broadcasted_iota(jnp.int32, (t, t), 1)).astype(F32)
    b_all = jnp.dot(tri, logf, precision=lax.Precision.HIGHEST, preferred_element_type=F32)
    tid = lax.broadcasted_iota(jnp.int32, (HG_DIAG, 1), 0)

    for h in range(n_head):
        hs = slice(h * dk, (h + 1) * dk)
        bh, qh, kh, vh = b_all[:, hs], q_all[:, hs], k_all[:, hs], v_all[:, hs]
        st = st_s[h]
        o_s[...] = lax.dot_general((qh * jnp.exp(bh)).astype(BF16), st.astype(BF16),
                                   (((1,), (1,)), ((), ())), preferred_element_type=F32)
        half = t // 2
        while half >= HG_DIAG:
            for lo in range(0, t, 2 * half):
                mid = lo + half
                piv = bh[mid - 1:mid]
                qt = (qh[mid:mid + half] * jnp.exp(bh[mid:mid + half] - piv)).astype(BF16)
                kt = (kh[lo:mid] * jnp.exp(piv - bh[lo:mid])).astype(BF16)
                att = lax.dot_general(qt, kt, (((1,), (1,)), ((), ())), preferred_element_type=F32)
                o_s[mid:mid + half] += _dot(att.astype(BF16), vh[lo:mid].astype(BF16))
            half //= 2
        for lo in range(0, t, HG_DIAG):
            bb, qq = bh[lo:lo + HG_DIAG], qh[lo:lo + HG_DIAG]
            kk, vv = kh[lo:lo + HG_DIAG], vh[lo:lo + HG_DIAG]
            acc = jnp.zeros((HG_DIAG, dk), F32)
            for s in range(HG_DIAG):
                dec = jnp.exp(jnp.minimum(bb - bb[s:s + 1], 0.0))
                r = jnp.sum(qq * kk[s:s + 1] * dec, axis=-1, keepdims=True)
                acc = acc + jnp.where(tid >= s, r, 0.0) * vv[s:s + 1]
            o_s[lo:lo + HG_DIAG] += acc
        b_end = bh[t - 1:t]
        kt = (kh * jnp.exp(b_end - bh)).astype(BF16)
        st_s[h] = st * jnp.exp(b_end) + lax.dot_general(
            vh.astype(BF16), kt, (((0,), (0,)), ((), ())), preferred_element_type=F32)
        o = o_s[...]
        o = o * lax.rsqrt(jnp.mean(o * o, axis=-1, keepdims=True) + EPS) * gn_ref[:, hs]
        o_ref[:, hs] = o * _sigmoid(g_in[:, hs])

    @pl.when(c == n_chunk - 1)
    def _():
        for h in range(n_head):
            sout_ref[0, h] = st_s[h].T


def _hgrn(z2, s0, lb_logits, gnorm, *, layer, t_chunk, n_chunk, valid_len):
    rows, zc = z2.shape
    n, n_head, dk, dv = s0.shape
    d_hg = zc // 4
    return pl.pallas_call(
        functools.partial(_hgrn_kernel, layer=layer, t_chunk=t_chunk, n_chunk=n_chunk,
                          valid_len=valid_len),
        grid=(rows // t_chunk,),
        in_specs=[pl.BlockSpec((t_chunk, zc), lambda i: (i, 0)),
                  pl.BlockSpec((1, n_head, dk, dv), lambda i: (i // n_chunk, 0, 0, 0)),
                  _full_spec(lb_logits), _full_spec(gnorm)],
        out_specs=[pl.BlockSpec((t_chunk, d_hg), lambda i: (i, 0)),
                   pl.BlockSpec((1, n_head, dk, dv), lambda i: (i // n_chunk, 0, 0, 0))],
        out_shape=[jax.ShapeDtypeStruct((rows, d_hg), F32),
                   jax.ShapeDtypeStruct(s0.shape, F32)],
        scratch_shapes=[pltpu.VMEM((n_head, dv, dk), F32), pltpu.VMEM((t_chunk, dv), F32)],
        compiler_params=_cparams(),
        name="hgrn2",
    )(z2, s0, lb_logits, gnorm)


def _ln_swish(c, g, b):
    mu = jnp.mean(c, axis=-1, keepdims=True)
    xc = c - mu
    y = xc * lax.rsqrt(jnp.mean(xc * xc, axis=-1, keepdims=True) + EPS) * g + b
    return y * _sigmoid(y)


def _conv_seq_kernel(u_ref, st_ref, w_ref, b_ref, g_ref, beta_ref, o_ref, ext_s, *, nb, tm):
    n_tap = w_ref.shape[0]
    halo = (n_tap - 1) * nb
    ch = u_ref.shape[1]

    @pl.when(pl.program_id(0) == 0)
    def _():
        ext_s[0:halo] = st_ref[...]

    ext_s[halo:halo + tm] = u_ref[...]

    def block(rb, _):
        r0 = pl.multiple_of(rb * CONV_ROWS, CONV_ROWS)
        acc = jnp.broadcast_to(b_ref[...], (CONV_ROWS, ch))
        for k in range(n_tap):
            acc = acc + w_ref[k:k + 1, :] * ext_s[pl.ds(r0 + k * nb, CONV_ROWS), :]
        o_ref[pl.ds(r0, CONV_ROWS), :] = _ln_swish(acc, g_ref[...], beta_ref[...]).astype(o_ref.dtype)
        return 0

    lax.fori_loop(0, tm // CONV_ROWS, block, 0)
    ext_s[0:halo] = ext_s[tm:tm + halo]


def _conv_seq(u_tm, st_tm, w, b, g, beta, *, nb, tm):
    rows, ch = u_tm.shape
    halo = st_tm.shape[0]
    small = [st_tm, w, b, g, beta]
    return pl.pallas_call(
        functools.partial(_conv_seq_kernel, nb=nb, tm=tm),
        grid=(rows // tm,),
        in_specs=[pl.BlockSpec((tm, ch), lambda i: (i, 0))] + [_full_spec(a) for a in small],
        out_specs=pl.BlockSpec((tm, ch), lambda i: (i, 0)),
        out_shape=jax.ShapeDtypeStruct((rows, ch), BF16),
        scratch_shapes=[pltpu.VMEM((halo + tm, ch), F32)],
        compiler_params=_cparams(),
        name="conf_conv_seq",
    )(u_tm, *small)


def _conv_step_kernel(u_ref, st_ref, w_ref, b_ref, g_ref, beta_ref, o_ref, c_s):
    n_prev = st_ref.shape[1]
    w_prev = w_ref[0:n_prev, :]
    for n in range(st_ref.shape[0]):
        c_s[n:n + 1, :] = jnp.sum(st_ref[n] * w_prev, axis=0, keepdims=True)
    c = c_s[...] + u_ref[...] * w_ref[n_prev:n_prev + 1, :] + b_ref[...]
    o_ref[...] = _ln_swish(c, g_ref[...], beta_ref[...]).astype(o_ref.dtype)


def _conv_step(u2, st, w, b, g, beta):
    n, ch = u2.shape
    n_prev = st.shape[1]
    small = [w, b, g, beta]
    return pl.pallas_call(
        _conv_step_kernel,
        grid=(n // SUBLANES,),
        in_specs=[pl.BlockSpec((SUBLANES, ch), lambda i: (i, 0)),
                  pl.BlockSpec((SUBLANES, n_prev, ch), lambda i: (i, 0, 0))]
                 + [_full_spec(a) for a in small],
        out_specs=pl.BlockSpec((SUBLANES, ch), lambda i: (i, 0)),
        out_shape=jax.ShapeDtypeStruct((n, ch), F32),
        scratch_shapes=[pltpu.VMEM((SUBLANES, ch), F32)],
        compiler_params=_cparams(),
        name="conf_conv_step",
    )(u2, st, *small)


def _softmax_rows(s):
    e = jnp.exp(s - jnp.max(s, axis=-1, keepdims=True))
    return e / jnp.sum(e, axis=-1, keepdims=True)


def _attn_seq_kernel(q_ref, k_ref, v_ref, o_ref, *, n_head):
    dh = q_ref.shape[1] // n_head
    scale = dh ** -0.5
    for h in range(n_head):
        hs = slice(h * dh, (h + 1) * dh)
        s = lax.dot_general(q_ref[:, hs], k_ref[0, :, hs].astype(BF16),
                            (((1,), (1,)), ((), ())), preferred_element_type=F32) * scale
        p = _softmax_rows(s).astype(BF16)
        o_ref[:, hs] = _dot(p, v_ref[0, :, hs].astype(BF16)).astype(o_ref.dtype)


def _attn_seq(q2, k3, v3, *, tm, tiles_per_batch, n_head):
    rows, d = q2.shape
    n_mem = k3.shape[1]
    kv_spec = pl.BlockSpec((1, n_mem, d), lambda i: (i // tiles_per_batch, 0, 0))
    return pl.pallas_call(
        functools.partial(_attn_seq_kernel, n_head=n_head),
        grid=(rows // tm,),
        in_specs=[pl.BlockSpec((tm, d), lambda i: (i, 0)), kv_spec, kv_spec],
        out_specs=pl.BlockSpec((tm, d), lambda i: (i, 0)),
        out_shape=jax.ShapeDtypeStruct((rows, d), BF16),
        compiler_params=_cparams(),
        name="xattn_seq",
    )(q2, k3, v3)


def _attn_step_kernel(q_ref, k_ref, v_ref, o_ref, *, n_head):
    d = q_ref.shape[2]
    dh = d // n_head
    scale = dh ** -0.5
    for n in range(q_ref.shape[0]):
        q = jnp.broadcast_to(q_ref[n], (SUBLANES, d)).astype(BF16)
        for h in range(n_head):
            hs = slice(h * dh, (h + 1) * dh)
            s = lax.dot_general(q[:, hs], k_ref[n, :, hs].astype(BF16),
                                (((1,), (1,)), ((), ())), preferred_element_type=F32) * scale
            p = _softmax_rows(s).astype(BF16)
            o = _dot(p, v_ref[n, :, hs].astype(BF16))
            o_ref[n, :, hs] = o[0:1]


def _attn_step(q3, k3, v3, *, nbt, n_head):
    n, _, d = q3.shape
    n_mem = k3.shape[1]
    kv_spec = pl.BlockSpec((nbt, n_mem, d), lambda i: (i, 0, 0))
    q_spec = pl.BlockSpec((nbt, 1, d), lambda i: (i, 0, 0))
    return pl.pallas_call(
        functools.partial(_attn_step_kernel, n_head=n_head),
        grid=(n // nbt,),
        in_specs=[q_spec, kv_spec, kv_spec],
        out_specs=q_spec,
        out_shape=jax.ShapeDtypeStruct((n, 1, d), F32),
        compiler_params=_cparams(),
        name="xattn_step",
    )(q3, k3, v3)


def _ffn_kernel(*refs, seq, tiles_per_batch):
    if seq:
        (x_ref, gpre_ref, gpost_ref, win_ref, wdw_ref, bdw_ref, wout_ref, init_ref,
         o_ref, alast_ref, h_s, acc_s, carry_s) = refs
    else:
        (x_ref, gpre_ref, gpost_ref, win_ref, wdw_ref, bdw_ref, wout_ref, p0_ref, p1_ref,
         o_ref, alast_ref, h_s, acc_s) = refs
    tm = x_ref.shape[0]
    d_ff = wout_ref.shape[0]
    x = x_ref[...]
    h_s[...] = _rms(x, gpre_ref[...]).astype(BF16)
    acc_s[...] = jnp.zeros_like(acc_s)
    if seq:
        @pl.when(pl.program_id(0) % tiles_per_batch == 0)
        def _():
            carry_s[...] = init_ref[0]
        rid = lax.broadcasted_iota(jnp.int32, (tm, FFN_COLS), 0)

    for c in range(d_ff // FFN_COLS):
        cs = slice(c * FFN_COLS, (c + 1) * FFN_COLS)
        h = h_s[...]
        a = _dot(h, win_ref[:, cs])
        b = _dot(h, win_ref[:, d_ff + c * FFN_COLS:d_ff + (c + 1) * FFN_COLS])
        if seq:
            prev = carry_s[:, cs]
            p1 = prev[SUBLANES - 1:SUBLANES]
            p2 = prev[SUBLANES - 2:SUBLANES - 1]
            a1 = jnp.where(rid == 0, p1, pltpu.roll(a, 1, axis=0))
            a2 = jnp.where(rid == 0, p2, jnp.where(rid == 1, p1, pltpu.roll(a, 2, axis=0)))
            tail = a[tm - SUBLANES:tm]
            carry_s[:, cs] = tail
            alast_ref[0, :, cs] = tail
        else:
            a2 = p0_ref[:, cs]
            a1 = p1_ref[:, cs]
            alast_ref[:, cs] = a
        conv = (wdw_ref[0:1, cs] * a2 + wdw_ref[1:2, cs] * a1 + wdw_ref[2:3, cs] * a
                + bdw_ref[:, cs])
        act = (_gelu(conv) * b).astype(BF16)
        acc_s[...] += _dot(act, wout_ref[cs, :])

    o_ref[...] = x + _rms(acc_s[...], gpost_ref[...])


def _ffn(x2, gpre, gpost, win, wdw, bdw, wout, buf, *, seq, tm, tiles_per_batch):
    rows, d = x2.shape
    d_ff = wout.shape[0]
    small = [gpre, gpost, win, wdw, bdw, wout]
    in_specs = ([pl.BlockSpec((tm, d), lambda i: (i, 0))] + [_full_spec(a) for a in small[:2]]
                + [_resident_spec(win)] + [_full_spec(a) for a in small[3:5]] + [_resident_spec(wout)])
    args = [x2] + small
    scratch = [pltpu.VMEM((tm, d), BF16), pltpu.VMEM((tm, d), F32)]
    if seq:
        n_batch = rows // (tm * tiles_per_batch)
        alast_spec = pl.BlockSpec((1, SUBLANES, d_ff), lambda i: (i // tiles_per_batch, 0, 0))
        alast_shape = jax.ShapeDtypeStruct((n_batch, SUBLANES, d_ff), F32)
        in_specs.append(alast_spec)
        args.append(jnp.pad(buf, ((0, 0), (SUBLANES - buf.shape[1], 0), (0, 0))))
        scratch.append(pltpu.VMEM((SUBLANES, d_ff), F32))
    else:
        in_specs += [pl.BlockSpec((tm, d_ff), lambda i: (i, 0))] * 2
        args += [buf[:, 0], buf[:, 1]]
        alast_spec = pl.BlockSpec((tm, d_ff), lambda i: (i, 0))
        alast_shape = jax.ShapeDtypeStruct((rows, d_ff), F32)
    return pl.pallas_call(
        functools.partial(_ffn_kernel, seq=seq, tiles_per_batch=tiles_per_batch),
        grid=(rows // tm,),
        in_specs=in_specs,
        out_specs=[pl.BlockSpec((tm, d), lambda i: (i, 0)), alast_spec],
        out_shape=[jax.ShapeDtypeStruct((rows, d), F32), alast_shape],
        scratch_shapes=scratch,
        compiler_params=_cparams(),
        name="conv_ffn",
    )(*args)


def _trunk(x, mem_k, mem_v, s5_re, s5_im, hg_s, conf_buf, ffn_buf, w, *, seq):
    n, l, d = x.shape
    depth = w['norm_gains'].shape[0]
    if seq:
        rows, tm = n * l, min(ROW_TILE, l)
        tpb = l // tm
        nb_tm = n
        tm_map = lambda i: (i % tpb, i // tpb)
    else:
        rows, tm, tpb = n, n, 1
        nb_tm = n
        tm_map = lambda i: (0, 0)
    row_map = lambda i: (i, 0)
    x2 = x.reshape(rows, d)
    new_re, new_im, new_hg, new_conf, new_ffn = [], [], [], [], []

    def tm_shape(c):
        return (l, n * c) if seq else (rows, c)

    for li in range(depth):
        g = w['norm_gains'][li][:, None, :]
        j = li // 2
        if li % 2 == 0:
            d_s5 = w['s5_d'].shape[1]
            d_rest = w['w_ab_in'].shape[2] - d_s5
            u_tm, z_rest = _norm_matmul(
                x2, g[NG_MIX_PRE], w['w_ab_in'][j],
                [(0, d_s5, tm_shape(d_s5), (tm, d_s5), tm_map, F32),
                 (d_s5, d_rest, (rows, d_rest), (tm, d_rest), row_map, F32)], tm=tm)
            n_state = s5_re.shape[2] * s5_re.shape[3]
            tc = min(S5_CHUNK, l) if seq else 1
            ya_tm, hr, hi = _s5(
                u_tm.reshape(l * n if seq else rows, d_s5),
                s5_re[j].reshape(n, n_state), s5_im[j].reshape(n, n_state),
                *w['s5_prep'][j], w['s5_d'][j][None], w['s5_w_glu'][j], w['s5_b_glu'][j][None],
                nb=nb_tm, tc=tc)
            new_re.append(hr.reshape(s5_re.shape[1:]))
            new_im.append(hi.reshape(s5_im.shape[1:]))
            if seq:
                t_chunk = min(HG_CHUNK, l)
                z_h, n_chunk, valid = z_rest, l // t_chunk, t_chunk
            else:
                t_chunk, n_chunk, valid = HG_DIAG, 1, 1
                z_h = jnp.pad(z_rest[:, None, :], ((0, 0), (0, t_chunk - 1), (0, 0))).reshape(n * t_chunk, d_rest)
            yb, hs = _hgrn(z_h, hg_s[j], w['hg_lb_logits'], w['hg_gnorm'][j][None],
                           layer=j, t_chunk=t_chunk, n_chunk=n_chunk, valid_len=valid)
            if not seq:
                yb = yb.reshape(n, t_chunk, -1)[:, 0]
            new_hg.append(hs)
            d_hg = yb.shape[1]
            ya_in = ya_tm.reshape(tm_shape(d_s5))
            x2 = _proj_res([(ya_in, (tm, d_s5), tm_map), (yb, (tm, d_hg), row_map)],
                           [w['w_ab_out'][j][:d_s5], w['w_ab_out'][j][d_s5:]],
                           x2, g[NG_MIX_POST], tm=tm)
        else:
            d_conf = w['w_conf_out'].shape[1]
            (u_tm,) = _norm_matmul(
                x2, g[NG_MIX_PRE], w['w_conf_in'][j],
                [(0, d_conf, tm_shape(d_conf), (tm, d_conf), tm_map, F32)], tm=tm, glu=True)
            cw = (w['conf_dw'][j], w['conf_dw_b'][j][None], w['conf_ln_g'][j][None], w['conf_ln_b'][j][None])
            n_prev = conf_buf.shape[2]
            if seq:
                u_rows = u_tm.reshape(l * n, d_conf)
                st_tm = conf_buf[j].transpose(1, 0, 2).reshape(n_prev * n, d_conf)
                c_tm = _conv_seq(u_rows, st_tm, *cw, nb=n, tm=tm)
                ext_tail = u_rows[(l - n_prev) * n:]
                new_conf.append(ext_tail.reshape(n_prev, n, d_conf).transpose(1, 0, 2))
                c_in = c_tm.reshape(tm_shape(d_conf))
            else:
                c_in = _conv_step(u_tm, conf_buf[j], *cw)
                new_conf.append(jnp.concatenate([conf_buf[j][:, 1:], u_tm[:, None, :]], axis=1))
            x2 = _proj_res([(c_in, (tm, d_conf), tm_map)], [w['w_conf_out'][j]],
                           x2, g[NG_MIX_POST], tm=tm)

        (q2,) = _norm_matmul(x2, g[NG_X_PRE], w['w_xq'][li],
                             [(0, d, (rows, d), (tm, d), row_map, BF16 if seq else F32)], tm=tm)
        k3 = mem_k[li].reshape(n, -1, d)
        v3 = mem_v[li].reshape(n, -1, d)
        if seq:
            a2 = _attn_seq(q2, k3, v3, tm=tm, tiles_per_batch=tpb, n_head=X_HEADS)
        else:
            a2 = _attn_step(q2[:, None, :], k3, v3, nbt=4, n_head=X_HEADS).reshape(rows, d)
        x2 = _proj_res([(a2, (tm, d), row_map)], [w['w_xo'][li]], x2, g[NG_X_POST], tm=tm)

        fw = (g[NG_FFN_PRE], g[NG_FFN_POST], w['w_ffn_in'][li], w['ffn_dw'][li],
              w['ffn_dw_b'][li][None], w['w_ffn_out'][li])
        x2, a_last = _ffn(x2, *fw, ffn_buf[li], seq=seq, tm=tm, tiles_per_batch=tpb)
        if seq:
            new_ffn.append(a_last[:, SUBLANES - ffn_buf.shape[2]:])
        else:
            new_ffn.append(jnp.concatenate([ffn_buf[li][:, 1:], a_last[:, None, :]], axis=1))

    return (x2.reshape(n, l, d), jnp.stack(new_re), jnp.stack(new_im), jnp.stack(new_hg),
            jnp.stack(new_conf), jnp.stack(new_ffn))


def kernel(x_prompt, x_sample, cache_mem_k, cache_mem_v, state_s5_re, state_s5_im, state_hgrn, state_conf, state_ffn, mem_prompt, norm_gains, w_ab_in, w_ab_out, s5_lambda_re, s5_lambda_im, s5_b_re, s5_b_im, s5_c_re, s5_c_im, s5_d, s5_log_dt, s5_w_glu, s5_b_glu, hg_lb_logits, hg_gnorm, w_conf_in, conf_dw, conf_dw_b, conf_ln_g, conf_ln_b, w_conf_out, w_xq, w_xk, w_xv, w_xo, w_ffn_in, ffn_dw, ffn_dw_b, w_ffn_out):
    depth = norm_gains.shape[0]
    n_ab = w_ab_in.shape[0]
    n_c = w_conf_in.shape[0]
    bf = lambda a: a.astype(BF16)

    s5_prep = []
    for j in range(n_ab):
        ar, ai, bbr_t, bbi_t = _s5_discretise(s5_lambda_re[j], s5_lambda_im[j], s5_log_dt[j],
                                              s5_b_re[j], s5_b_im[j])
        s5_prep.append((ar, ai) + _s5_block_weights(bbr_t, bbi_t, s5_c_re[j], s5_c_im[j]))

    w = dict(norm_gains=norm_gains, w_ab_in=bf(w_ab_in), w_ab_out=bf(w_ab_out), s5_prep=s5_prep,
             s5_d=s5_d, s5_w_glu=bf(s5_w_glu), s5_b_glu=s5_b_glu,
             hg_lb_logits=hg_lb_logits, hg_gnorm=hg_gnorm,
             w_conf_in=bf(w_conf_in), conf_dw=conf_dw, conf_dw_b=conf_dw_b,
             conf_ln_g=conf_ln_g, conf_ln_b=conf_ln_b, w_conf_out=bf(w_conf_out),
             w_xq=bf(w_xq), w_xo=bf(w_xo), w_ffn_in=bf(w_ffn_in), ffn_dw=ffn_dw,
             ffn_dw_b=ffn_dw_b, w_ffn_out=bf(w_ffn_out))

    nb, n_mem, d = mem_prompt.shape
    mem2 = mem_prompt.reshape(nb * n_mem, d)
    tm_mem = min(ROW_TILE, nb * n_mem)
    mk, mv = [], []
    for li in range(depth):
        w_kv = jnp.concatenate([bf(w_xk[li]), bf(w_xv[li])], axis=1)
        row_map = lambda i: (i, 0)
        k2, v2 = _norm_matmul(mem2, norm_gains[li, NG_MEM][None], w_kv,
                              [(0, d, (nb * n_mem, d), (tm_mem, d), row_map, F32),
                               (d, d, (nb * n_mem, d), (tm_mem, d), row_map, F32)], tm=tm_mem)
        mk.append(k2.reshape(nb, n_mem, X_HEADS, d // X_HEADS))
        mv.append(v2.reshape(nb, n_mem, X_HEADS, d // X_HEADS))
    p_mem_k = jnp.stack(mk)
    p_mem_v = jnp.stack(mv)

    nbp = x_prompt.shape[0]
    z_s5 = jnp.zeros((n_ab, nbp) + state_s5_re.shape[2:], F32)
    z_hg = jnp.zeros((n_ab, nbp) + state_hgrn.shape[2:], F32)
    z_conf = jnp.zeros((n_c, nbp) + state_conf.shape[2:], F32)
    z_ffn = jnp.zeros((depth, nbp) + state_ffn.shape[2:], F32)
    y_prompt, p_re, p_im, p_hg, p_conf, p_ffn = _trunk(
        x_prompt, p_mem_k, p_mem_v, z_s5, z_s5, z_hg, z_conf, z_ffn, w, seq=True)

    y_sample, s_re, s_im, s_hg, s_conf, s_ffn = _trunk(
        x_sample, cache_mem_k, cache_mem_v, state_s5_re, state_s5_im, state_hgrn,
        state_conf, state_ffn, w, seq=False)

    return (y_prompt, y_sample, p_re, p_im, p_hg, p_conf, p_ffn, p_mem_k, p_mem_v,
            s_re, s_im, s_hg, s_conf, s_ffn)
```

```python
import functools

import jax
import jax.numpy as jnp
from jax import lax
from jax.experimental import pallas as pl
from jax.experimental.pallas import tpu as pltpu

F32 = jnp.float32
BF16 = jnp.bfloat16
EPS = 1e-6

N_NORMS = 7
NG_MIX_PRE, NG_MIX_POST, NG_X_PRE, NG_X_POST, NG_FFN_PRE, NG_FFN_POST, NG_MEM = range(N_NORMS)

HG_DK = 128
X_HEADS = 4

LANES = 128
SUBLANES = 8
VMEM_LIMIT = 56 * 1024 * 1024

ROW_TILE = 512
HG_CHUNK = 64
HG_SUB = 2
HG_DIAG = 16
S5_CHUNK = 64
S5_LANES = 512
FFN_COLS = 256
CONV_ROWS = 16
STEP_BATCH = 4


def _cparams(n_axes=1):
    return pltpu.CompilerParams(dimension_semantics=("arbitrary",) * n_axes,
                                vmem_limit_bytes=VMEM_LIMIT)


def _rms(x, g):
    return x * lax.rsqrt(jnp.mean(x * x, axis=-1, keepdims=True) + EPS) * g


def _gelu(x):
    return 0.5 * x * (1.0 + jnp.tanh(0.7978845608028654 * (x + 0.044715 * (x * x * x))))


def _sigmoid(x):
    return 1.0 / (1.0 + jnp.exp(-x))


def _dot(a, b):
    return jnp.dot(a, b, preferred_element_type=F32)


def _full_spec(arr):
    nd = arr.ndim
    return pl.BlockSpec(arr.shape, lambda *_: (0,) * nd)


def _resident_spec(arr):
    nd = arr.ndim
    return pl.BlockSpec(arr.shape, lambda *_: (0,) * nd, pipeline_mode=pl.Buffered(1))


def _load_rows(ref, kind):
    if kind == 'tm':
        return jnp.concatenate([ref[:, b, :] for b in range(ref.shape[1])], axis=0)
    v = ref[...]
    return v.reshape(-1, v.shape[-1]) if v.ndim == 3 else v


def _store_rows(ref, kind, z):
    if kind == 'tm':
        tt, nb, _ = ref.shape
        for b in range(nb):
            ref[:, b, :] = z[b * tt:(b + 1) * tt].astype(ref.dtype)
    elif kind == 'heads':
        nbb, n_mem, n_head, dh = ref.shape
        for h in range(n_head):
            ref[:, :, h, :] = z[:, h * dh:(h + 1) * dh].reshape(nbb, n_mem, dh).astype(ref.dtype)
    else:
        ref[...] = z.reshape(ref.shape).astype(ref.dtype)


def _norm_matmul_kernel(x_ref, g_ref, w_ref, *o_refs, col_slices, kinds, glu):
    h = _rms(_load_rows(x_ref, 'rows'), g_ref[...]).astype(BF16)
    n_half = w_ref.shape[1] // 2
    done = {}
    for o_ref, (c0, cw), kind in zip(o_refs, col_slices, kinds):
        if (c0, cw) not in done:
            z = _dot(h, w_ref[:, c0:c0 + cw])
            if glu:
                z = z * _sigmoid(_dot(h, w_ref[:, n_half + c0:n_half + c0 + cw]))
            done[(c0, cw)] = z
        _store_rows(o_ref, kind, done[(c0, cw)])


def _norm_matmul(x, g, w, *, grid, x_spec, g_spec, w_spec, outs, glu=False):
    return pl.pallas_call(
        functools.partial(_norm_matmul_kernel, col_slices=tuple(o['cols'] for o in outs),
                          kinds=tuple(o['kind'] for o in outs), glu=glu),
        grid=grid,
        in_specs=[x_spec, g_spec, w_spec],
        out_specs=[o['spec'] for o in outs],
        out_shape=[jax.ShapeDtypeStruct(o['shape'], o['dtype']) for o in outs],
        compiler_params=_cparams(len(grid)),
        name="norm_matmul",
    )(x, g, w)


def _proj_res_kernel(*refs, kinds):
    n_in = len(kinds)
    a_refs, w_refs = refs[:n_in], refs[n_in:2 * n_in]
    x_ref, g_ref, o_ref = refs[2 * n_in:]
    y = None
    for a_ref, w_ref, kind in zip(a_refs, w_refs, kinds):
        part = _dot(_load_rows(a_ref, kind).astype(BF16), w_ref[...])
        y = part if y is None else y + part
    x = x_ref[...]
    o_ref[...] = x + _rms(y, g_ref[...]).reshape(x.shape)


def _proj_res(a_list, w_list, x, g, *, grid, x_spec):
    return pl.pallas_call(
        functools.partial(_proj_res_kernel, kinds=tuple(a[2] for a in a_list)),
        grid=grid,
        in_specs=([a[1] for a in a_list] + [_full_spec(w) for w in w_list]
                  + [x_spec, _full_spec(g)]),
        out_specs=x_spec,
        out_shape=jax.ShapeDtypeStruct(x.shape, F32),
        compiler_params=_cparams(len(grid)),
        name="proj_res",
    )(*[a[0] for a in a_list], *w_list, x, g)


def _s5_disc_kernel(lr_ref, li_ref, ldt_ref, br_ref, bi_ref, ar_ref, ai_ref, bbr_ref, bbi_ref):
    lr, li = lr_ref[...], li_ref[...]
    dt = jnp.exp(ldt_ref[...])
    mag = jnp.exp(lr * dt)
    ar, ai = mag * jnp.cos(li * dt), mag * jnp.sin(li * dt)
    den = lr * lr + li * li
    er = ((ar - 1.0) * lr + ai * li) / den
    ei = (ai * lr - (ar - 1.0) * li) / den
    br, bi = br_ref[...], bi_ref[...]
    ar_ref[...] = ar
    ai_ref[...] = ai
    bbr_ref[...] = er * br - ei * bi
    bbi_ref[...] = er * bi + ei * br


def _s5_discretise(lam_re, lam_im, log_dt, b_re, b_im):
    g, p = lam_re.shape
    c = b_re.shape[-1]
    gp = g * p
    args = (lam_re.reshape(1, gp), lam_im.reshape(1, gp),
            jnp.repeat(log_dt, p).reshape(1, gp),
            b_re.transpose(2, 0, 1).reshape(c, gp), b_im.transpose(2, 0, 1).reshape(c, gp))
    row = jax.ShapeDtypeStruct((1, gp), F32)
    mat = jax.ShapeDtypeStruct((c, gp), F32)
    return pl.pallas_call(
        _s5_disc_kernel,
        in_specs=[_full_spec(a) for a in args],
        out_specs=[pl.BlockSpec((1, gp), lambda: (0, 0))] * 2 + [pl.BlockSpec((c, gp), lambda: (0, 0))] * 2,
        out_shape=[row, row, mat, mat],
        name="s5_discretise",
    )(*args)


def _s5_block_weights(bbr_t, bbi_t, c_re, c_im):
    g, c, p = c_re.shape
    gps = LANES // c
    n_slab = g // gps
    eye = jnp.eye(gps, dtype=F32)

    def in_w(bt):
        b4 = bt.reshape(c, n_slab, gps, p).transpose(1, 2, 0, 3)
        return jnp.einsum('jgcp,gh->jgchp', b4, eye).reshape(n_slab, gps * c, gps * p)

    def out_w(cm):
        c4 = cm.reshape(n_slab, gps, c, p)
        return jnp.einsum('jgcp,gh->jgphc', c4, eye).reshape(n_slab, gps * p, gps * c)

    wb = jnp.concatenate([in_w(bbr_t), in_w(bbi_t)], axis=-1).astype(BF16)
    return wb, out_w(c_re).astype(BF16), out_w(c_im).astype(BF16)


def _s5_kernel(u_ref, h0r_ref, h0i_ref, ar_ref, ai_ref, wb_ref, wcr_ref, wci_ref, d_ref,
               wglu_ref, bglu_ref, y_ref, hr_ref, hi_ref, xr_s, xi_s, cr_s, ci_s, y_s, *, nb, tc):
    n_slab, _, two_sc = wb_ref.shape
    sc = two_sc // 2
    n_state = n_slab * sc

    @pl.when(pl.program_id(0) == 0)
    def _():
        cr_s[...] = h0r_ref[...]
        ci_s[...] = h0i_ref[...]

    u = u_ref[...]
    ub = u.astype(BF16)
    for j in range(n_slab):
        bu = _dot(ub[:, j * LANES:(j + 1) * LANES], wb_ref[j])
        xr_s[:, j * sc:(j + 1) * sc] = bu[:, :sc]
        xi_s[:, j * sc:(j + 1) * sc] = bu[:, sc:]

    for rt in range(nb // SUBLANES):
        for cb in range(n_state // S5_LANES):
            rows = slice(rt * SUBLANES, (rt + 1) * SUBLANES)
            cols = slice(cb * S5_LANES, (cb + 1) * S5_LANES)
            ar = jnp.broadcast_to(ar_ref[:, cols], (SUBLANES, S5_LANES))
            ai = jnp.broadcast_to(ai_ref[:, cols], (SUBLANES, S5_LANES))

            def step(t, carry, cols=cols, rt=rt, ar=ar, ai=ai):
                xr, xi = carry
                r0 = pl.multiple_of(t * nb + rt * SUBLANES, SUBLANES)
                nr = (ar * xr - ai * xi) + xr_s[pl.ds(r0, SUBLANES), cols]
                ni = (ar * xi + ai * xr) + xi_s[pl.ds(r0, SUBLANES), cols]
                xr_s[pl.ds(r0, SUBLANES), cols] = nr
                xi_s[pl.ds(r0, SUBLANES), cols] = ni
                return nr, ni

            carry = (cr_s[rows, cols], ci_s[rows, cols])
            if tc == 1:
                carry = step(0, carry)
            else:
                carry = lax.fori_loop(0, tc, step, carry, unroll=4)
            cr_s[rows, cols] = carry[0]
            ci_s[rows, cols] = carry[1]

    hr_ref[...] = cr_s[...]
    hi_ref[...] = ci_s[...]

    for j in range(n_slab):
        xr = xr_s[:, j * sc:(j + 1) * sc].astype(BF16)
        xi = xi_s[:, j * sc:(j + 1) * sc].astype(BF16)
        y_s[:, j * LANES:(j + 1) * LANES] = _dot(xr, wcr_ref[j]) - _dot(xi, wci_ref[j])
    y = y_s[...] + d_ref[...] * u
    z = _gelu(y)
    gate = _dot(z.astype(BF16), wglu_ref[...]) + bglu_ref[...]
    y_ref[...] = z * _sigmoid(gate)


def _s5(u_tm, h0r, h0i, ar, ai, wb, wcr, wci, d, wglu, bglu, *, nb, tc):
    rows, ch = u_tm.shape
    n_state = h0r.shape[1]
    blk = tc * nb
    small = [h0r, h0i, ar, ai, wb, wcr, wci, d, wglu, bglu]
    state = jax.ShapeDtypeStruct((nb, n_state), F32)
    return pl.pallas_call(
        functools.partial(_s5_kernel, nb=nb, tc=tc),
        grid=(rows // blk,),
        in_specs=[pl.BlockSpec((blk, ch), lambda i: (i, 0))] + [_full_spec(a) for a in small],
        out_specs=[pl.BlockSpec((blk, ch), lambda i: (i, 0)),
                   pl.BlockSpec((nb, n_state), lambda i: (0, 0)),
                   pl.BlockSpec((nb, n_state), lambda i: (0, 0))],
        out_shape=[jax.ShapeDtypeStruct((rows, ch), F32), state, state],
        scratch_shapes=[pltpu.VMEM((blk, n_state), F32), pltpu.VMEM((blk, n_state), F32),
                        pltpu.VMEM((nb, n_state), F32), pltpu.VMEM((nb, n_state), F32),
                        pltpu.VMEM((blk, ch), F32)],
        compiler_params=_cparams(),
        name="s5",
    )(u_tm, *small)


def _hgrn_kernel(z_ref, s0_ref, lbl_ref, gn_ref, o_ref, sout_ref, st_s, o_s, *,
                 layer, t_chunk, n_sub, n_step, valid_len):
    c = pl.program_id(0) % n_step
    n_head, dk, _ = s0_ref.shape
    d_hg = n_head * dk
    t = t_chunk
    rows = n_sub * t

    @pl.when(c == 0)
    def _():
        for h in range(n_head):
            st_s[h] = s0_ref[h].T

    lg = lbl_ref[...]
    e = jnp.exp(lg - jnp.max(lg, axis=0, keepdims=True))
    sm = e / jnp.sum(e, axis=0, keepdims=True)
    lb = jnp.sum(sm[:layer + 1], axis=0, keepdims=True) - sm[0:1]

    q_in = z_ref[:, 0:d_hg]
    f_in = z_ref[:, d_hg:2 * d_hg]
    v_all = z_ref[:, 2 * d_hg:3 * d_hg]
    g_in = z_ref[:, 3 * d_hg:4 * d_hg]
    f = lb + (1.0 - lb) * _sigmoid(f_in)
    logf = jnp.log(f)
    k_all = 1.0 - f
    if valid_len < rows:
        live = lax.broadcasted_iota(jnp.int32, (rows, 1), 0) < valid_len
        logf = jnp.where(live, logf, 0.0)
        k_all = jnp.where(live, k_all, 0.0)
    q_all = q_in * _sigmoid(q_in)
    tri = (lax.broadcasted_iota(jnp.int32, (t, t), 0)
           >= lax.broadcasted_iota(jnp.int32, (t, t), 1)).astype(F32)
    b_all = [jnp.dot(tri, logf[u * t:(u + 1) * t], precision=lax.Precision.HIGHEST,
                     preferred_element_type=F32) for u in range(n_sub)]
    tid = lax.broadcasted_iota(jnp.int32, (HG_DIAG, 1), 0)

    for h in range(n_head):
        hs = slice(h * dk, (h + 1) * dk)
        st = st_s[h]
        for u in range(n_sub):
            r0 = u * t
            bh = b_all[u][:, hs]
            qh, kh, vh = q_all[r0:r0 + t, hs], k_all[r0:r0 + t, hs], v_all[r0:r0 + t, hs]
            o_s[h, r0:r0 + t] = lax.dot_general((qh * jnp.exp(bh)).astype(BF16), st.astype(BF16),
                                                (((1,), (1,)), ((), ())), preferred_element_type=F32)
            half = t // 2
            while half >= HG_DIAG:
                for lo in range(0, t, 2 * half):
                    mid = lo + half
                    piv = bh[mid - 1:mid]
                    qt = (qh[mid:mid + half] * jnp.exp(bh[mid:mid + half] - piv)).astype(BF16)
                    kt = (kh[lo:mid] * jnp.exp(piv - bh[lo:mid])).astype(BF16)
                    att = lax.dot_general(qt, kt, (((1,), (1,)), ((), ())), preferred_element_type=F32)
                    o_s[h, r0 + mid:r0 + mid + half] += _dot(att.astype(BF16), vh[lo:mid].astype(BF16))
                half //= 2
            for lo in range(0, t, HG_DIAG):
                bb, qq = bh[lo:lo + HG_DIAG], qh[lo:lo + HG_DIAG]
                kk, vv = kh[lo:lo + HG_DIAG], vh[lo:lo + HG_DIAG]
                acc = jnp.zeros((HG_DIAG, dk), F32)
                for s in range(HG_DIAG):
                    dec = jnp.exp(jnp.minimum(bb - bb[s:s + 1], 0.0))
                    r = jnp.sum(qq * kk[s:s + 1] * dec, axis=-1, keepdims=True)
                    acc = acc + jnp.where(tid >= s, r, 0.0) * vv[s:s + 1]
                o_s[h, r0 + lo:r0 + lo + HG_DIAG] += acc
            b_end = bh[t - 1:t]
            kt = (kh * jnp.exp(b_end - bh)).astype(BF16)
            st = st * jnp.exp(b_end) + lax.dot_general(
                vh.astype(BF16), kt, (((0,), (0,)), ((), ())), preferred_element_type=F32)
        st_s[h] = st
        o = o_s[h]
        o = o * lax.rsqrt(jnp.mean(o * o, axis=-1, keepdims=True) + EPS) * gn_ref[:, hs]
        o_ref[:, hs] = o * _sigmoid(g_in[:, hs])

    @pl.when(c == n_step - 1)
    def _():
        for h in range(n_head):
            sout_ref[0, h] = st_s[h].T


def _hgrn(z2, s0_all, lb_logits, gnorm, *, layer, t_chunk, n_sub, n_step, valid_len):
    rows, zc = z2.shape
    _, n, n_head, dk, dv = s0_all.shape
    d_hg = zc // 4
    blk = n_sub * t_chunk
    return pl.pallas_call(
        functools.partial(_hgrn_kernel, layer=layer, t_chunk=t_chunk, n_sub=n_sub, n_step=n_step,
                          valid_len=valid_len),
        grid=(rows // blk,),
        in_specs=[pl.BlockSpec((blk, zc), lambda i: (i, 0)),
                  pl.BlockSpec((None, None, n_head, dk, dv), lambda i: (layer, i // n_step, 0, 0, 0)),
                  _full_spec(lb_logits), _full_spec(gnorm)],
        out_specs=[pl.BlockSpec((blk, d_hg), lambda i: (i, 0)),
                   pl.BlockSpec((1, n_head, dk, dv), lambda i: (i // n_step, 0, 0, 0))],
        out_shape=[jax.ShapeDtypeStruct((rows, d_hg), F32),
                   jax.ShapeDtypeStruct((n, n_head, dk, dv), F32)],
        scratch_shapes=[pltpu.VMEM((n_head, dv, dk), F32), pltpu.VMEM((n_head, blk, dv), F32)],
        compiler_params=_cparams(),
        name="hgrn2",
    )(z2, s0_all, lb_logits, gnorm)


def _ln_swish(c, g, b):
    mu = jnp.mean(c, axis=-1, keepdims=True)
    xc = c - mu
    y = xc * lax.rsqrt(jnp.mean(xc * xc, axis=-1, keepdims=True) + EPS) * g + b
    return y * _sigmoid(y)


def _conv_seq_kernel(u_ref, st_ref, w_ref, b_ref, g_ref, beta_ref, o_ref, ext_s, c_s, *, nb, tm):
    n_tap = w_ref.shape[0] // SUBLANES
    halo = (n_tap - 1) * nb
    ch = u_ref.shape[1]

    @pl.when(pl.program_id(0) == 0)
    def _():
        ext_s[0:halo] = st_ref[...]

    ext_s[halo:halo + tm] = u_ref[...]

    def block(rb, _):
        r0 = pl.multiple_of(rb * CONV_ROWS, CONV_ROWS)
        bias = jnp.broadcast_to(b_ref[...], (SUBLANES, ch))
        accs = [bias] * (CONV_ROWS // SUBLANES)
        for k in range(n_tap):
            wk = w_ref[k * SUBLANES:(k + 1) * SUBLANES, :]
            for r in range(len(accs)):
                accs[r] = accs[r] + wk * ext_s[pl.ds(r0 + k * nb + r * SUBLANES, SUBLANES), :]
        for r, acc in enumerate(accs):
            c_s[pl.ds(r0 + r * SUBLANES, SUBLANES), :] = acc
        return 0

    lax.fori_loop(0, tm // CONV_ROWS, block, 0)
    ext_s[0:halo] = ext_s[tm:tm + halo]
    o_ref[...] = _ln_swish(c_s[...], g_ref[...], beta_ref[...]).astype(o_ref.dtype)


def _conv_seq(u_tm, st_tm, w, b, g, beta, *, nb, tm):
    rows, ch = u_tm.shape
    halo = st_tm.shape[0]
    small = [st_tm, w, b, g, beta]
    return pl.pallas_call(
        functools.partial(_conv_seq_kernel, nb=nb, tm=tm),
        grid=(rows // tm,),
        in_specs=[pl.BlockSpec((tm, ch), lambda i: (i, 0))] + [_full_spec(a) for a in small],
        out_specs=pl.BlockSpec((tm, ch), lambda i: (i, 0)),
        out_shape=jax.ShapeDtypeStruct((rows, ch), BF16),
        scratch_shapes=[pltpu.VMEM((halo + tm, ch), F32), pltpu.VMEM((tm, ch), F32)],
        compiler_params=_cparams(),
        name="conf_conv_seq",
    )(u_tm, *small)


def _conv_step_kernel(u_ref, st_ref, w_ref, b_ref, g_ref, beta_ref, o_ref, c_s):
    n_prev = st_ref.shape[1]
    w_prev = w_ref[0:n_prev, :]
    for n in range(st_ref.shape[0]):
        c_s[n:n + 1, :] = jnp.sum(st_ref[n] * w_prev, axis=0, keepdims=True)
    c = c_s[...] + u_ref[...] * w_ref[n_prev:n_prev + 1, :] + b_ref[...]
    o_ref[...] = _ln_swish(c, g_ref[...], beta_ref[...]).astype(o_ref.dtype)


def _conv_step(u2, st_all, w, b, g, beta, *, layer):
    n, ch = u2.shape
    n_prev = st_all.shape[2]
    small = [w, b, g, beta]
    return pl.pallas_call(
        _conv_step_kernel,
        grid=(n // SUBLANES,),
        in_specs=[pl.BlockSpec((SUBLANES, ch), lambda i: (i, 0)),
                  pl.BlockSpec((None, SUBLANES, n_prev, ch), lambda i: (layer, i, 0, 0))]
                 + [_full_spec(a) for a in small],
        out_specs=pl.BlockSpec((SUBLANES, ch), lambda i: (i, 0)),
        out_shape=jax.ShapeDtypeStruct((n, ch), F32),
        scratch_shapes=[pltpu.VMEM((SUBLANES, ch), F32)],
        compiler_params=_cparams(),
        name="conf_conv_step",
    )(u2, st_all, *small)


def _softmax_rows(s):
    e = jnp.exp(s - jnp.max(s, axis=-1, keepdims=True))
    return e / jnp.sum(e, axis=-1, keepdims=True)


def _attn_seq_kernel(q_ref, k_ref, v_ref, o_ref, *, n_head):
    dh = q_ref.shape[1] // n_head
    for h in range(n_head):
        hs = slice(h * dh, (h + 1) * dh)
        s = lax.dot_general(q_ref[:, hs], k_ref[:, hs], (((1,), (1,)), ((), ())),
                            preferred_element_type=F32) * dh ** -0.5
        o_ref[:, hs] = _dot(_softmax_rows(s).astype(BF16), v_ref[:, hs]).astype(o_ref.dtype)


def _attn_seq(q2, k3, v3, *, layer, tm, tiles_per_batch, n_head):
    rows, d = q2.shape
    n_mem = k3.shape[1] // (rows // (tm * tiles_per_batch))
    kv_spec = pl.BlockSpec((None, n_mem, d), lambda i: (layer, i // tiles_per_batch, 0))
    return pl.pallas_call(
        functools.partial(_attn_seq_kernel, n_head=n_head),
        grid=(rows // tm,),
        in_specs=[pl.BlockSpec((tm, d), lambda i: (i, 0)), kv_spec, kv_spec],
        out_specs=pl.BlockSpec((tm, d), lambda i: (i, 0)),
        out_shape=jax.ShapeDtypeStruct((rows, d), BF16),
        compiler_params=_cparams(),
        name="xattn_seq",
    )(q2, k3, v3)


def _attn_step_kernel(q_ref, k_ref, v_ref, o_ref):
    nbt, n_mem, n_head, dh = k_ref.shape
    d = n_head * dh
    cols = n_mem * n_head
    rid = lax.broadcasted_iota(jnp.int32, (SUBLANES, dh), 0)
    own = (lax.broadcasted_iota(jnp.int32, (SUBLANES, cols), 1) % n_head
           == lax.broadcasted_iota(jnp.int32, (SUBLANES, cols), 0))
    scores = []
    for n in range(nbt):
        q8 = jnp.broadcast_to(q_ref[n], (SUBLANES, d))
        qm = jnp.zeros((SUBLANES, dh), F32)
        for h in range(n_head):
            qm = jnp.where(rid == h, q8[:, h * dh:(h + 1) * dh], qm)
        k2 = k_ref[n].reshape(cols, dh).astype(BF16)
        s = lax.dot_general(qm.astype(BF16), k2, (((1,), (1,)), ((), ())),
                            preferred_element_type=F32) * dh ** -0.5
        scores.append(jnp.where(own, s, -1e30))
    p_all = _softmax_rows(jnp.concatenate(scores, axis=0)).astype(BF16)
    for n in range(nbt):
        v2 = v_ref[n].reshape(cols, dh).astype(BF16)
        o = _dot(p_all[n * SUBLANES:(n + 1) * SUBLANES], v2)
        for h in range(n_head):
            o_ref[n, :, h * dh:(h + 1) * dh] = o[h:h + 1]


def _attn_step(q3, k5, v5, *, layer):
    n, _, d = q3.shape
    _, _, n_mem, n_head, dh = k5.shape
    kv_spec = pl.BlockSpec((None, STEP_BATCH, n_mem, n_head, dh), lambda i: (layer, i, 0, 0, 0))
    q_spec = pl.BlockSpec((STEP_BATCH, 1, d), lambda i: (i, 0, 0))
    return pl.pallas_call(
        _attn_step_kernel,
        grid=(n // STEP_BATCH,),
        in_specs=[q_spec, kv_spec, kv_spec],
        out_specs=q_spec,
        out_shape=jax.ShapeDtypeStruct((n, 1, d), F32),
        compiler_params=_cparams(),
        name="xattn_step",
    )(q3, k5, v5)


def _ffn_kernel(*refs, seq, tiles_per_batch):
    if seq:
        (x_ref, gpre_ref, gpost_ref, win_ref, wdw_ref, bdw_ref, wout_ref, init_ref,
         o_ref, alast_ref, h_s, acc_s, ab_s, act_s, carry_s) = refs
    else:
        (x_ref, gpre_ref, gpost_ref, win_ref, wdw_ref, bdw_ref, wout_ref, p0_ref, p1_ref,
         o_ref, alast_ref, h_s, acc_s, ab_s, act_s) = refs
    tm = x_ref.shape[0]
    d_ff = wout_ref.shape[0]
    n_col = d_ff // FFN_COLS
    x = x_ref[...]
    h_s[...] = _rms(x, gpre_ref[...]).astype(BF16)
    if seq:
        @pl.when(pl.program_id(0) % tiles_per_batch == 0)
        def _():
            carry_s[...] = init_ref[0]
        rid = lax.broadcasted_iota(jnp.int32, (tm, FFN_COLS), 0)

    def expand(c):
        h = h_s[...]
        ab_s[c % 2, :, :FFN_COLS] = _dot(h, win_ref[:, c * FFN_COLS:(c + 1) * FFN_COLS])
        ab_s[c % 2, :, FFN_COLS:] = _dot(h, win_ref[:, d_ff + c * FFN_COLS:d_ff + (c + 1) * FFN_COLS])

    def activate(c):
        cs = slice(c * FFN_COLS, (c + 1) * FFN_COLS)
        a = ab_s[c % 2, :, :FFN_COLS]
        b = ab_s[c % 2, :, FFN_COLS:]
        if seq:
            prev = carry_s[:, cs]
            p1 = prev[SUBLANES - 1:SUBLANES]
            p2 = prev[SUBLANES - 2:SUBLANES - 1]
            a1 = jnp.where(rid == 0, p1, pltpu.roll(a, 1, axis=0))
            a2 = jnp.where(rid == 0, p2, jnp.where(rid == 1, p1, pltpu.roll(a, 2, axis=0)))
            tail = a[tm - SUBLANES:tm]
            carry_s[:, cs] = tail
            alast_ref[0, :, cs] = tail
        else:
            a2 = p0_ref[:, cs]
            a1 = p1_ref[:, cs]
            alast_ref[:, cs] = a
        conv = (wdw_ref[0:1, cs] * a2 + wdw_ref[1:2, cs] * a1 + wdw_ref[2:3, cs] * a
                + bdw_ref[:, cs])
        act_s[c % 2] = (_gelu(conv) * b).astype(BF16)

    def contract(c):
        part = _dot(act_s[c % 2], wout_ref[c * FFN_COLS:(c + 1) * FFN_COLS, :])
        if c == 0:
            acc_s[...] = part
        else:
            acc_s[...] += part

    expand(0)
    for c in range(n_col):
        if c + 1 < n_col:
            expand(c + 1)
        activate(c)
        if c >= 1:
            contract(c - 1)
    contract(n_col - 1)

    o_ref[...] = x + _rms(acc_s[...], gpost_ref[...])


def _ffn(x2, gpre, gpost, win, wdw, bdw, wout, buf, *, seq, tm, tiles_per_batch):
    rows, d = x2.shape
    d_ff = wout.shape[0]
    small = [gpre, gpost, win, wdw, bdw, wout]
    in_specs = ([pl.BlockSpec((tm, d), lambda i: (i, 0))] + [_full_spec(a) for a in small[:2]]
                + [_resident_spec(win)] + [_full_spec(a) for a in small[3:5]] + [_resident_spec(wout)])
    args = [x2] + small
    scratch = [pltpu.VMEM((tm, d), BF16), pltpu.VMEM((tm, d), F32),
               pltpu.VMEM((2, tm, 2 * FFN_COLS), F32), pltpu.VMEM((2, tm, FFN_COLS), BF16)]
    if seq:
        n_batch = rows // (tm * tiles_per_batch)
        alast_spec = pl.BlockSpec((1, SUBLANES, d_ff), lambda i: (i // tiles_per_batch, 0, 0))
        alast_shape = jax.ShapeDtypeStruct((n_batch, SUBLANES, d_ff), F32)
        in_specs.append(alast_spec)
        args.append(jnp.pad(buf, ((0, 0), (SUBLANES - buf.shape[1], 0), (0, 0))))
        scratch.append(pltpu.VMEM((SUBLANES, d_ff), F32))
    else:
        in_specs += [pl.BlockSpec((tm, d_ff), lambda i: (i, 0))] * 2
        args += [buf[:, 0], buf[:, 1]]
        alast_spec = pl.BlockSpec((tm, d_ff), lambda i: (i, 0))
        alast_shape = jax.ShapeDtypeStruct((rows, d_ff), F32)
    return pl.pallas_call(
        functools.partial(_ffn_kernel, seq=seq, tiles_per_batch=tiles_per_batch),
        grid=(rows // tm,),
        in_specs=in_specs,
        out_specs=[pl.BlockSpec((tm, d), lambda i: (i, 0)), alast_spec],
        out_shape=[jax.ShapeDtypeStruct((rows, d), F32), alast_shape],
        scratch_shapes=scratch,
        compiler_params=_cparams(),
        name="conv_ffn",
    )(*args)


def _trunk(x, mem_k, mem_v, s5_re, s5_im, hg_s, conf_buf, ffn_buf, w, *, seq):
    n, l, d = x.shape
    depth = w['norm_gains'].shape[0]
    row_map = lambda i: (i, 0)
    if seq:
        rows, tm = n * l, min(ROW_TILE, l)
        tpb = l // tm
        tt = tm // n
        grid_tm = (l // tt,)
        x_tm_spec = pl.BlockSpec((n, tt, d), lambda i: (0, i, 0))
    else:
        rows, tm, tpb = n, n, 1
    grid_rows = (rows // tm,)
    x_row_spec = pl.BlockSpec((tm, d), row_map)
    x2 = x.reshape(rows, d)
    new_re, new_im, new_hg, new_conf, new_ffn = [], [], [], [], []

    def mix_in(w_in, outs, glu=False):
        specs = []
        for c0, cw, time_major in outs:
            if seq and time_major:
                specs.append(dict(cols=(c0, cw), shape=(l, n, cw), dtype=F32, kind='tm',
                                  spec=pl.BlockSpec((tt, n, cw), lambda i: (i, 0, 0))))
            elif seq:
                specs.append(dict(cols=(c0, cw), shape=(n, l, cw), dtype=F32, kind='rows',
                                  spec=pl.BlockSpec((n, tt, cw), lambda i: (0, i, 0))))
            else:
                specs.append(dict(cols=(c0, cw), shape=(rows, cw), dtype=F32, kind='rows',
                                  spec=pl.BlockSpec((tm, cw), row_map)))
        xin, grid, x_spec = ((x2.reshape(n, l, d), grid_tm, x_tm_spec) if seq
                             else (x2, grid_rows, x_row_spec))
        return _norm_matmul(xin, g[NG_MIX_PRE], w_in, grid=grid, x_spec=x_spec,
                            g_spec=_full_spec(g[NG_MIX_PRE]), w_spec=_full_spec(w_in),
                            outs=specs, glu=glu)

    def mix_out(a_list, w_list):
        ins = []
        for a, cw, time_major in a_list:
            if seq and time_major:
                ins.append((a.reshape(l, n, cw), pl.BlockSpec((tt, n, cw), lambda i: (i, 0, 0)), 'tm'))
            elif seq:
                ins.append((a.reshape(n, l, cw), pl.BlockSpec((n, tt, cw), lambda i: (0, i, 0)), 'rows'))
            else:
                ins.append((a, pl.BlockSpec((tm, cw), row_map), 'rows'))
        if seq:
            out = _proj_res(ins, w_list, x2.reshape(n, l, d), g[NG_MIX_POST], grid=grid_tm, x_spec=x_tm_spec)
            return out.reshape(rows, d)
        return _proj_res(ins, w_list, x2, g[NG_MIX_POST], grid=grid_rows, x_spec=x_row_spec)

    for li in range(depth):
        g = w['norm_gains'][li][:, None, :]
        j = li // 2
        if li % 2 == 0:
            d_s5 = w['s5_d'].shape[1]
            d_rest = w['w_ab_in'].shape[2] - d_s5
            u_tm, z_rest = mix_in(w['w_ab_in'][j], [(0, d_s5, True), (d_s5, d_rest, False)])
            n_state = s5_re.shape[2] * s5_re.shape[3]
            ya_tm, hr, hi = _s5(
                u_tm.reshape(rows, d_s5),
                s5_re[j].reshape(n, n_state), s5_im[j].reshape(n, n_state),
                *w['s5_prep'][j], w['s5_d'][j][None], w['s5_w_glu'][j], w['s5_b_glu'][j][None],
                nb=n, tc=min(S5_CHUNK, l) if seq else 1)
            new_re.append(hr.reshape(s5_re.shape[1:]))
            new_im.append(hi.reshape(s5_im.shape[1:]))
            z_h = z_rest.reshape(rows, d_rest)
            if seq:
                t_chunk = min(HG_CHUNK, l)
                n_sub = min(HG_SUB, l // t_chunk)
                n_step, valid = l // (t_chunk * n_sub), t_chunk * n_sub
            else:
                t_chunk, n_sub, n_step, valid = HG_DIAG, 1, 1, 1
                z_h = jnp.pad(z_h[:, None, :], ((0, 0), (0, t_chunk - 1), (0, 0))).reshape(n * t_chunk, d_rest)
            yb, hs = _hgrn(z_h, hg_s, w['hg_lb_logits'], w['hg_gnorm'][j][None],
                           layer=j, t_chunk=t_chunk, n_sub=n_sub, n_step=n_step, valid_len=valid)
            if not seq:
                yb = yb.reshape(n, t_chunk, -1)[:, 0]
            new_hg.append(hs)
            d_hg = yb.shape[1]
            x2 = mix_out([(ya_tm, d_s5, True), (yb, d_hg, False)],
                         [w['w_ab_out'][j][:d_s5], w['w_ab_out'][j][d_s5:]])
        else:
            d_conf = w['w_conf_out'].shape[1]
            (u_tm,) = mix_in(w['w_conf_in'][j], [(0, d_conf, True)], glu=True)
            cw = (w['conf_dw'][j], w['conf_dw_b'][j][None], w['conf_ln_g'][j][None], w['conf_ln_b'][j][None])
            n_prev = conf_buf.shape[2]
            if seq:
                u_rows = u_tm.reshape(rows, d_conf)
                st_tm = conf_buf[j].transpose(1, 0, 2).reshape(n_prev * n, d_conf)
                c_in = _conv_seq(u_rows, st_tm, jnp.repeat(cw[0], SUBLANES, axis=0), *cw[1:], nb=n, tm=tm)
                new_conf.append(u_rows[(l - n_prev) * n:].reshape(n_prev, n, d_conf).transpose(1, 0, 2))
            else:
                c_in = _conv_step(u_tm, conf_buf, *cw, layer=j)
                new_conf.append(jnp.concatenate([conf_buf[j][:, 1:], u_tm[:, None, :]], axis=1))
            x2 = mix_out([(c_in, d_conf, True)], [w['w_conf_out'][j]])

        (q2,) = _norm_matmul(
            x2, g[NG_X_PRE], w['w_xq'][li], grid=grid_rows, x_spec=x_row_spec,
            g_spec=_full_spec(g[NG_X_PRE]), w_spec=_full_spec(w['w_xq'][li]),
            outs=[dict(cols=(0, d), shape=(rows, d), dtype=BF16 if seq else F32, kind='rows',
                       spec=pl.BlockSpec((tm, d), row_map))])
        if seq:
            a2 = _attn_seq(q2, mem_k, mem_v, layer=li, tm=tm, tiles_per_batch=tpb, n_head=X_HEADS)
        else:
            a2 = _attn_step(q2[:, None, :], mem_k, mem_v, layer=li).reshape(rows, d)
        x2 = _proj_res([(a2, pl.BlockSpec((tm, d), row_map), 'rows')], [w['w_xo'][li]],
                       x2, g[NG_X_POST], grid=grid_rows, x_spec=x_row_spec)

        fw = (g[NG_FFN_PRE], g[NG_FFN_POST], w['w_ffn_in'][li], w['ffn_dw'][li],
              w['ffn_dw_b'][li][None], w['w_ffn_out'][li])
        x2, a_last = _ffn(x2, *fw, ffn_buf[li], seq=seq, tm=tm, tiles_per_batch=tpb)
        if seq:
            new_ffn.append(a_last[:, SUBLANES - ffn_buf.shape[2]:])
        else:
            new_ffn.append(jnp.concatenate([ffn_buf[li][:, 1:], a_last[:, None, :]], axis=1))

    return (x2.reshape(n, l, d), jnp.stack(new_re), jnp.stack(new_im), jnp.stack(new_hg),
            jnp.stack(new_conf), jnp.stack(new_ffn))


def _memory_kv(mem_prompt, gains, w_kv):
    nb, n_mem, d = mem_prompt.shape
    depth = w_kv.shape[0]
    rows = nb * n_mem
    tm = min(ROW_TILE, rows)
    dh = d // X_HEADS
    kv_shape = (depth, nb, n_mem, X_HEADS, dh)
    kv_spec = pl.BlockSpec((None, tm // n_mem, n_mem, X_HEADS, dh), lambda li, i: (li, i, 0, 0, 0))
    op_spec = pl.BlockSpec((None, tm, d), lambda li, i: (li, i, 0))
    outs = ([dict(cols=(c0, d), shape=kv_shape, dtype=F32, kind='heads', spec=kv_spec) for c0 in (0, d)]
            + [dict(cols=(c0, d), shape=(depth, rows, d), dtype=BF16, kind='rows', spec=op_spec)
               for c0 in (0, d)])
    return _norm_matmul(
        mem_prompt.reshape(rows, d), gains, w_kv, grid=(depth, rows // tm),
        x_spec=pl.BlockSpec((tm, d), lambda li, i: (i, 0)),
        g_spec=pl.BlockSpec((None, 1, d), lambda li, i: (li, 0, 0)),
        w_spec=pl.BlockSpec((None, d, 2 * d), lambda li, i: (li, 0, 0)),
        outs=outs)


def kernel(x_prompt, x_sample, cache_mem_k, cache_mem_v, state_s5_re, state_s5_im, state_hgrn, state_conf, state_ffn, mem_prompt, norm_gains, w_ab_in, w_ab_out, s5_lambda_re, s5_lambda_im, s5_b_re, s5_b_im, s5_c_re, s5_c_im, s5_d, s5_log_dt, s5_w_glu, s5_b_glu, hg_lb_logits, hg_gnorm, w_conf_in, conf_dw, conf_dw_b, conf_ln_g, conf_ln_b, w_conf_out, w_xq, w_xk, w_xv, w_xo, w_ffn_in, ffn_dw, ffn_dw_b, w_ffn_out):
    depth = norm_gains.shape[0]
    n_ab = w_ab_in.shape[0]
    n_c = w_conf_in.shape[0]
    bf = lambda a: a.astype(BF16)

    s5_prep = []
    for j in range(n_ab):
        ar, ai, bbr_t, bbi_t = _s5_discretise(s5_lambda_re[j], s5_lambda_im[j], s5_log_dt[j],
                                              s5_b_re[j], s5_b_im[j])
        s5_prep.append((ar, ai) + _s5_block_weights(bbr_t, bbi_t, s5_c_re[j], s5_c_im[j]))

    w = dict(norm_gains=norm_gains, w_ab_in=bf(w_ab_in), w_ab_out=bf(w_ab_out), s5_prep=s5_prep,
             s5_d=s5_d, s5_w_glu=bf(s5_w_glu), s5_b_glu=s5_b_glu,
             hg_lb_logits=hg_lb_logits, hg_gnorm=hg_gnorm,
             w_conf_in=bf(w_conf_in), conf_dw=conf_dw, conf_dw_b=conf_dw_b,
             conf_ln_g=conf_ln_g, conf_ln_b=conf_ln_b, w_conf_out=bf(w_conf_out),
             w_xq=bf(w_xq), w_xo=bf(w_xo), w_ffn_in=bf(w_ffn_in), ffn_dw=ffn_dw,
             ffn_dw_b=ffn_dw_b, w_ffn_out=bf(w_ffn_out))

    p_mem_k, p_mem_v, k_op, v_op = _memory_kv(mem_prompt, norm_gains[:, NG_MEM][:, None, :],
                                              jnp.concatenate([bf(w_xk), bf(w_xv)], axis=2))

    nbp = x_prompt.shape[0]
    z_s5 = jnp.zeros((n_ab, nbp) + state_s5_re.shape[2:], F32)
    z_hg = jnp.zeros((n_ab, nbp) + state_hgrn.shape[2:], F32)
    z_conf = jnp.zeros((n_c, nbp) + state_conf.shape[2:], F32)
    z_ffn = jnp.zeros((depth, nbp) + state_ffn.shape[2:], F32)
    y_prompt, p_re, p_im, p_hg, p_conf, p_ffn = _trunk(
        x_prompt, k_op, v_op, z_s5, z_s5, z_hg, z_conf, z_ffn, w, seq=True)

    y_sample, s_re, s_im, s_hg, s_conf, s_ffn = _trunk(
        x_sample, cache_mem_k, cache_mem_v, state_s5_re, state_s5_im, state_hgrn,
        state_conf, state_ffn, w, seq=False)

    return (y_prompt, y_sample, p_re, p_im, p_hg, p_conf, p_ffn, p_mem_k, p_mem_v,
            s_re, s_im, s_hg, s_conf, s_ffn)
```
